```python
import math
import jax, jax.numpy as jnp
from jax import lax
import numpy as np

D_MODEL = 1024
BATCH = 2
SEQ = 16384
DEPTH = 2

HEAD_DIM = 64
ROPE_THETA = 10000.0
BLOCK_Q = 128
N_SUPER = 16
NEG_INF = -1e30
EPS = 1e-6

DIL_PATTERNS = ((128, 1), (512, 4), (2048, 16))
DIL_HEADS = 4
MLA_HEADS = 4
MLA_Q_RANK = 256
MLA_KV_RANK = 128
MLA_NOPE_DIM = 64
MLA_ROPE_DIM = 32
MLA_V_DIM = 64
SB_HEADS = 4
DIFF_HEADS = 4
DIFF_QK_DIM = 32
DIFF_V_DIM = 64
MEM_LEN = 256
CROSS_HEADS = 4
D_FF = 2048
N_EXPERTS = 8
TOP_K = 2
D_FF_EXPERT = 512
N_DENSE_LAYERS = (DEPTH + 1) // 2
N_MOE_LAYERS = DEPTH // 2

N_BRANCHES = 4
BRANCH_WIDTH = 256
A_COLS = 3 * DIL_HEADS * HEAD_DIM
B_COLS = MLA_Q_RANK + MLA_KV_RANK + MLA_ROPE_DIM
C_COLS = 3 * SB_HEADS * HEAD_DIM
D_COLS = 2 * DIFF_HEADS * 2 * DIFF_QK_DIM + DIFF_HEADS * DIFF_V_DIM
IN_COLS = A_COLS + B_COLS + C_COLS + D_COLS

kernel_name = 'hybrid_gated_multimixer_block'


def rms_norm(x, g):
    xf = x.astype(jnp.float32)
    y = xf * lax.rsqrt(jnp.mean(xf * xf, axis=-1, keepdims=True) + EPS)
    return (y * g.astype(jnp.float32)).astype(x.dtype)


def rope_tables(seq, dim, dtype):
    inv = 1.0 / (ROPE_THETA ** (jnp.arange(0, dim, 2, dtype=jnp.float32) / dim))
    ang = jnp.arange(seq, dtype=jnp.float32)[:, None] * inv[None, :]
    return jnp.cos(ang).astype(dtype), jnp.sin(ang).astype(dtype)


def apply_rope(x, cos, sin):
    half = x.shape[-1] // 2
    x1, x2 = x[..., :half], x[..., half:]
    if x.ndim == 4:
        cos, sin = cos[:, None, :], sin[:, None, :]
    return jnp.concatenate([x1 * cos - x2 * sin, x2 * cos + x1 * sin], axis=-1)


def causal_block_mask(idx, kl, strict=False):
    qpos = idx * BLOCK_Q + jnp.arange(BLOCK_Q)
    kpos = jnp.arange(kl)
    return (kpos[None, :] < qpos[:, None]) if strict else (kpos[None, :] <= qpos[:, None])


def causal_sweep(block_fn, q_args, kv_args):
    b, s = q_args[0].shape[:2]
    nb = s // BLOCK_Q
    n_seg = math.gcd(nb, N_SUPER)
    per = nb // n_seg
    outs = []
    for c in range(n_seg):
        start, end = c * per * BLOCK_Q, (c + 1) * per * BLOCK_Q
        kv = tuple(t[:, :end] for t in kv_args)
        blocks = tuple(jnp.swapaxes(t[:, start:end].reshape((b, per, BLOCK_Q) + t.shape[2:]), 0, 1) for t in q_args)
        idx = c * per + jnp.arange(per)
        o = lax.map(lambda a, kv=kv: block_fn(a[0], kv, *a[1]), (idx, blocks))
        outs.append(jnp.swapaxes(o, 0, 1).reshape((b, per * BLOCK_Q) + o.shape[3:]))
    return jnp.concatenate(outs, axis=1)


def dilated_window_attention(q, k, v, window, dilation):
    b, s, h, dh = q.shape
    span = window // dilation
    n_sub = s // dilation
    n_blk = -(-n_sub // BLOCK_Q)
    pad = n_blk * BLOCK_Q - n_sub

    def to_sub(t):
        t = t.reshape(b, n_sub, dilation, h, dh).transpose(0, 2, 1, 3, 4).reshape(b * dilation, n_sub, h, dh)
        t = jnp.pad(t, ((0, 0), (0, pad), (0, 0), (0, 0)))
        return t.reshape(b * dilation, n_blk, BLOCK_Q, h, dh)

    def with_prev(t):
        prev = jnp.pad(t[:, :-1], ((0, 0), (1, 0), (0, 0), (0, 0), (0, 0)))
        return jnp.concatenate([prev, t], axis=2)

    def from_sub(t):
        tail = t.shape[3:]
        t = t.reshape((b * dilation, n_blk * BLOCK_Q) + tail)[:, :n_sub]
        t = jnp.swapaxes(t.reshape((b, dilation, n_sub) + tail), 1, 2)
        return t.reshape((b, s) + tail)

    qs = to_sub(q)
    kb, vb = with_prev(to_sub(k)), with_prev(to_sub(v))
    scores = jnp.einsum('znqhd,znkhd->znhqk', qs, kb, preferred_element_type=jnp.float32) * (dh ** -0.5)
    qi = jnp.arange(BLOCK_Q)[:, None]
    kj = jnp.arange(2 * BLOCK_Q)[None, :]
    rel = BLOCK_Q + qi - kj
    band = (rel >= 0) & (rel <= span)
    valid_prev = (jnp.arange(n_blk)[:, None, None] > 0) | (kj[None] >= BLOCK_Q)
    mask = band[None] & valid_prev
    scores = jnp.where(mask[None, :, None], scores, NEG_INF)
    m = jnp.max(scores, axis=-1, keepdims=True)
    p = jnp.exp(scores - m)
    den = jnp.sum(p, axis=-1, keepdims=True)
    out = jnp.einsum('znhqk,znkhd->znqhd', (p / den).astype(v.dtype), vb)
    lse = jnp.swapaxes((m + jnp.log(den))[..., 0], 2, 3)
    return from_sub(out), from_sub(lse)


def dilated_branch(pa, cos, sin):
    b, s, _ = pa.shape
    pa = pa.reshape(b, s, 3, DIL_HEADS, HEAD_DIM)
    q = apply_rope(pa[:, :, 0], cos, sin)
    k = apply_rope(pa[:, :, 1], cos, sin)
    v = pa[:, :, 2]
    outs, lses = [], []
    for window, dilation in DIL_PATTERNS:
        o, lse = dilated_window_attention(q, k, v, window, dilation)
        outs.append(o)
        lses.append(lse)
    w = jax.nn.softmax(jnp.stack(lses), axis=0)
    o = jnp.einsum('gbsh,gbshd->bshd', w.astype(pa.dtype), jnp.stack(outs))
    return o.reshape(b, s, DIL_HEADS * HEAD_DIM)


def mla_branch(c_q, c_kv, k_rope_raw, q_norm, w_uq, kv_norm, w_ukv, cos, sin):
    b, s, _ = c_q.shape
    q = (rms_norm(c_q, q_norm) @ w_uq).reshape(b, s, MLA_HEADS, MLA_NOPE_DIM + MLA_ROPE_DIM)
    q_nope, q_rope = q[..., :MLA_NOPE_DIM], apply_rope(q[..., MLA_NOPE_DIM:], cos, sin)
    kv = (rms_norm(c_kv, kv_norm) @ w_ukv).reshape(b, s, MLA_HEADS, MLA_NOPE_DIM + MLA_V_DIM)
    k_nope, v = kv[..., :MLA_NOPE_DIM], kv[..., MLA_NOPE_DIM:]
    k_rope = apply_rope(k_rope_raw, cos, sin)
    scale = (MLA_NOPE_DIM + MLA_ROPE_DIM) ** -0.5

    def block(idx, kvs, qn, qr):
        kn, kr, vv = kvs
        sc = (jnp.einsum('bqhd,bkhd->bhqk', qn, kn, preferred_element_type=jnp.float32)
              + jnp.einsum('bqhr,bkr->bhqk', qr, kr, preferred_element_type=jnp.float32)) * scale
        sc = jnp.where(causal_block_mask(idx, kn.shape[1]), sc, NEG_INF)
        p = jax.nn.softmax(sc, axis=-1).astype(vv.dtype)
        return jnp.einsum('bhqk,bkhd->bqhd', p, vv)

    o = causal_sweep(block, (q_nope, q_rope), (k_nope, k_rope, v))
    return o.reshape(b, s, MLA_HEADS * MLA_V_DIM)


def stick_breaking_branch(pc):
    b, s, _ = pc.shape
    pc = pc.reshape(b, s, 3, SB_HEADS, HEAD_DIM)
    q, k, v = pc[:, :, 0], pc[:, :, 1], pc[:, :, 2]
    scale = HEAD_DIM ** -0.5

    def block(idx, kvs, qb):
        kk, vv = kvs
        z = jnp.einsum('bqhd,bkhd->bhqk', qb, kk, preferred_element_type=jnp.float32) * scale
        z = jnp.where(causal_block_mask(idx, kk.shape[1], strict=True), z, NEG_INF)
        log_1m_beta = jax.nn.log_sigmoid(-z)
        a = jnp.exp(z + lax.cumsum(log_1m_beta, axis=3, reverse=True))
        return jnp.einsum('bhqk,bkhd->bqhd', a.astype(vv.dtype), vv)

    o = causal_sweep(block, (q,), (k, v))
    return o.reshape(b, s, SB_HEADS * HEAD_DIM)


def diff_branch(pd, diff_lambda, diff_norm, lambda_init, cos, sin):
    b, s, _ = pd.shape
    qk_w = DIFF_HEADS * 2 * DIFF_QK_DIM
    q = apply_rope(pd[..., :qk_w].reshape(b, s, 2 * DIFF_HEADS, DIFF_QK_DIM), cos, sin).reshape(b, s, DIFF_HEADS, 2, DIFF_QK_DIM)
    k = apply_rope(pd[..., qk_w:2 * qk_w].reshape(b, s, 2 * DIFF_HEADS, DIFF_QK_DIM), cos, sin).reshape(b, s, DIFF_HEADS, 2, DIFF_QK_DIM)
    v = pd[..., 2 * qk_w:].reshape(b, s, DIFF_HEADS, DIFF_V_DIM)
    lf = diff_lambda.astype(jnp.float32)
    lam = jnp.exp(jnp.sum(lf[0] * lf[1])) - jnp.exp(jnp.sum(lf[2] * lf[3])) + lambda_init
    scale = DIFF_QK_DIM ** -0.5

    def block(idx, kvs, qb):
        kk, vv = kvs
        sc = jnp.einsum('bqhcd,bkhcd->bchqk', qb, kk, preferred_element_type=jnp.float32) * scale
        p = jax.nn.softmax(jnp.where(causal_block_mask(idx, kk.shape[1]), sc, NEG_INF), axis=-1)
        attn = (p[:, 0] - lam * p[:, 1]).astype(vv.dtype)
        return jnp.einsum('bhqk,bkhd->bqhd', attn, vv)

    o = causal_sweep(block, (q,), (k, v))
    o = rms_norm(o, diff_norm) * (1.0 - lambda_init)
    return o.reshape(b, s, DIFF_HEADS * DIFF_V_DIM)


def hybrid_mixer(h, w_in, mla_q_norm, mla_w_uq, mla_kv_norm, mla_w_ukv, diff_lambda, diff_norm,
                 w_branch, w_gate, b_gate, w_out, lambda_init, rope64, rope32):
    b, s, d = h.shape
    proj = h @ w_in
    pa, pb, pc, pd = jnp.split(proj, [A_COLS, A_COLS + B_COLS, A_COLS + B_COLS + C_COLS], axis=-1)
    c_q, c_kv, k_rope = jnp.split(pb, [MLA_Q_RANK, MLA_Q_RANK + MLA_KV_RANK], axis=-1)
    branches = (
        dilated_branch(pa, *rope64),
        mla_branch(c_q, c_kv, k_rope, mla_q_norm, mla_w_uq, mla_kv_norm, mla_w_ukv, *rope32),
        stick_breaking_branch(pc),
        diff_branch(pd, diff_lambda, diff_norm, lambda_init, *rope32),
    )
    gates = jax.nn.sigmoid(h @ w_gate + b_gate).reshape(b, s, N_BRANCHES, d)
    y = gates[:, :, 0] * (branches[0] @ w_branch[0])
    for i in range(1, N_BRANCHES):
        y = y + gates[:, :, i] * (branches[i] @ w_branch[i])
    return y @ w_out


def memory_cross_attention(h, mem_h, w_cq, w_ckv, w_co):
    b, s, _ = h.shape
    m = mem_h.shape[1]
    q = (h @ w_cq).reshape(b, s, CROSS_HEADS, HEAD_DIM)
    kv = (mem_h @ w_ckv).reshape(b, m, 2, CROSS_HEADS, HEAD_DIM)
    sc = jnp.einsum('bqhd,bmhd->bhqm', q, kv[:, :, 0], preferred_element_type=jnp.float32) * (HEAD_DIM ** -0.5)
    p = jax.nn.softmax(sc, axis=-1).astype(h.dtype)
    o = jnp.einsum('bhqm,bmhd->bqhd', p, kv[:, :, 1]).reshape(b, s, CROSS_HEADS * HEAD_DIM)
    return o @ w_co


def swiglu(h, w1, w3, w2):
    return (jax.nn.silu(h @ w1) * (h @ w3)) @ w2


def moe_swiglu(h, w_router, w1, w3, w2):
    logits = jnp.einsum('bsd,de->bse', h, w_router, preferred_element_type=jnp.float32)
    top_val, top_idx = lax.top_k(logits, TOP_K)
    top_w = jax.nn.softmax(top_val, axis=-1)
    combine = jnp.sum(jax.nn.one_hot(top_idx, N_EXPERTS, dtype=jnp.float32) * top_w[..., None], axis=-2).astype(h.dtype)
    out = combine[..., 0:1] * swiglu(h, w1[0], w3[0], w2[0])
    for e in range(1, N_EXPERTS):
        out = out + combine[..., e:e + 1] * swiglu(h, w1[e], w3[e], w2[e])
    return out


def _normal(key, shape, scale):
    return jax.random.normal(key, shape, dtype=jnp.float32) * scale


def _gain(key, shape):
    return 1.0 + _normal(key, shape, 0.05)


def setup_inputs(seed: int = 0) -> dict:
    key = jax.random.key(seed)
    ks = jax.random.split(key, 32)
    d = D_MODEL
    return {
        'x': _normal(ks[0], (BATCH, SEQ, d), 1.0),
        'mem': _normal(ks[1], (BATCH, MEM_LEN, d), 1.0),
        'mix_norm': _gain(ks[2], (DEPTH, d)),
        'w_in': _normal(ks[3], (DEPTH, d, IN_COLS), d ** -0.5),
        'mla_q_norm': _gain(ks[4], (DEPTH, MLA_Q_RANK)),
        'mla_w_uq': _normal(ks[5], (DEPTH, MLA_Q_RANK, MLA_HEADS * (MLA_NOPE_DIM + MLA_ROPE_DIM)), MLA_Q_RANK ** -0.5),
        'mla_kv_norm': _gain(ks[6], (DEPTH, MLA_KV_RANK)),
        'mla_w_ukv': _normal(ks[7], (DEPTH, MLA_KV_RANK, MLA_HEADS * (MLA_NOPE_DIM + MLA_V_DIM)), MLA_KV_RANK ** -0.5),
        'diff_lambda': _normal(ks[8], (DEPTH, 4, DIFF_QK_DIM), 0.1),
        'diff_norm': _gain(ks[9], (DEPTH, DIFF_V_DIM)),
        'w_branch': _normal(ks[10], (DEPTH, N_BRANCHES, BRANCH_WIDTH, d), BRANCH_WIDTH ** -0.5),
        'w_gate': _normal(ks[11], (DEPTH, d, N_BRANCHES * d), d ** -0.5),
        'b_gate': _normal(ks[12], (DEPTH, N_BRANCHES * d), 0.02),
        'w_out': _normal(ks[13], (DEPTH, d, d), d ** -0.5),
        'cross_norm': _gain(ks[14], (DEPTH, d)),
        'mem_norm': _gain(ks[15], (DEPTH, d)),
        'w_cq': _normal(ks[16], (DEPTH, d, CROSS_HEADS * HEAD_DIM), d ** -0.5),
        'w_ckv': _normal(ks[17], (DEPTH, d, 2 * CROSS_HEADS * HEAD_DIM), d ** -0.5),
        'w_co': _normal(ks[18], (DEPTH, CROSS_HEADS * HEAD_DIM, d), (CROSS_HEADS * HEAD_DIM) ** -0.5),
        'ffn_norm': _gain(ks[19], (DEPTH, d)),
        'dense_w1': _normal(ks[20], (N_DENSE_LAYERS, d, D_FF), d ** -0.5),
        'dense_w3': _normal(ks[21], (N_DENSE_LAYERS, d, D_FF), d ** -0.5),
        'dense_w2': _normal(ks[22], (N_DENSE_LAYERS, D_FF, d), D_FF ** -0.5),
        'moe_router': _normal(ks[23], (N_MOE_LAYERS, d, N_EXPERTS), d ** -0.5),
        'moe_w1': _normal(ks[24], (N_MOE_LAYERS, N_EXPERTS, d, D_FF_EXPERT), d ** -0.5),
        'moe_w3': _normal(ks[25], (N_MOE_LAYERS, N_EXPERTS, d, D_FF_EXPERT), d ** -0.5),
        'moe_w2': _normal(ks[26], (N_MOE_LAYERS, N_EXPERTS, D_FF_EXPERT, d), D_FF_EXPERT ** -0.5),
        'final_norm': _gain(ks[27], (d,)),
    }


def reference(x, mem, mix_norm, w_in, mla_q_norm, mla_w_uq, mla_kv_norm, mla_w_ukv, diff_lambda, diff_norm,
              w_branch, w_gate, b_gate, w_out, cross_norm, mem_norm, w_cq, w_ckv, w_co, ffn_norm,
              dense_w1, dense_w3, dense_w2, moe_router, moe_w1, moe_w3, moe_w2, final_norm):
    s = x.shape[1]
    rope64 = rope_tables(s, HEAD_DIM, x.dtype)
    rope32 = rope_tables(s, DIFF_QK_DIM, x.dtype)
    for layer in range(DEPTH):
        lambda_init = 0.8 - 0.6 * math.exp(-0.3 * layer)
        h = rms_norm(x, mix_norm[layer])
        x = x + hybrid_mixer(h, w_in[layer], mla_q_norm[layer], mla_w_uq[layer], mla_kv_norm[layer], mla_w_ukv[layer],
                             diff_lambda[layer], diff_norm[layer], w_branch[layer], w_gate[layer], b_gate[layer],
                             w_out[layer], lambda_init, rope64, rope32)
        x = x + memory_cross_attention(rms_norm(x, cross_norm[layer]), rms_norm(mem, mem_norm[layer]),
                                       w_cq[layer], w_ckv[layer], w_co[layer])
        hf = rms_norm(x, ffn_norm[layer])
        i = layer // 2
        if layer % 2 == 0:
            x = x + swiglu(hf, dense_w1[i], dense_w3[i], dense_w2[i])
        else:
            x = x + moe_swiglu(hf, moe_router[i], moe_w1[i], moe_w3[i], moe_w2[i])
    return rms_norm(x, final_norm)
```

```python
import functools
import math

import numpy as np
import jax
import jax.numpy as jnp
from jax import lax
from jax.experimental import pallas as pl
from jax.experimental.pallas import tpu as pltpu

F32 = jnp.float32
BF16 = jnp.bfloat16

HEAD_DIM = 64
ROPE_THETA = 10000.0
NEG_INF = -1e30
EPS = 1e-6
LOG2E = math.log2(math.e)

DIL_PATTERNS = ((128, 1), (512, 4), (2048, 16))
N_HEADS = 4
MLA_Q_RANK = 256
MLA_KV_RANK = 128
MLA_NOPE_DIM = 64
MLA_ROPE_DIM = 32
DIFF_QK_DIM = 32
N_EXPERTS = 8
D_FF_EXPERT = 512
LANES = 128

VMEM_LIMIT = 56 * 1024 * 1024


def _cparams(sem):
    return pltpu.CompilerParams(dimension_semantics=sem, vmem_limit_bytes=VMEM_LIMIT)


def _rms(x, g):
    return x * lax.rsqrt(jnp.mean(x * x, axis=-1, keepdims=True) + EPS) * g


def _dot(a, b):
    return jnp.dot(a, b, preferred_element_type=F32)


def _dot_nt(a, b):
    return lax.dot_general(a, b, (((1,), (1,)), ((), ())), preferred_element_type=F32)


def _sigmoid(x):
    return 1.0 / (1.0 + jnp.exp(-x))


def _lane(shape):
    return lax.broadcasted_iota(jnp.int32, shape, len(shape) - 1)


PROJ_COLS = 2816
SCALE_A = HEAD_DIM ** -0.5 * LOG2E
SCALE_B = (MLA_NOPE_DIM + MLA_ROPE_DIM) ** -0.5 * LOG2E
SCALE_C = HEAD_DIM ** -0.5
SCALE_D = DIFF_QK_DIM ** -0.5 * LOG2E


def _perm_w_in():
    a_off, b_off, c_off, d_off = 0, 768, 1184, 1952
    pair = np.zeros(256, np.int64)
    for p in range(2):
        for half in range(2):
            for j in range(2):
                for r in range(32):
                    pair[p * 128 + half * 64 + j * 32 + r] = (2 * p + j) * 64 + half * 32 + r
    unit = np.zeros(256, np.int64)
    for g in range(2):
        for half in range(2):
            for u in range(4):
                for r in range(16):
                    unit[g * 128 + half * 64 + u * 16 + r] = (4 * g + u) * 32 + half * 16 + r
    nat = np.arange(256)
    krope = np.concatenate([np.tile(np.arange(16), 4), np.tile(16 + np.arange(16), 4)])
    cols = np.concatenate([
        a_off + pair, a_off + 256 + pair, a_off + 512 + nat,
        d_off + unit, d_off + 256 + unit, d_off + 512 + nat,
        c_off + np.arange(768),
        b_off + np.arange(MLA_Q_RANK + MLA_KV_RANK),
        b_off + MLA_Q_RANK + MLA_KV_RANK + krope,
    ])
    assert cols.shape[0] == PROJ_COLS
    return cols


def _perm_w_uq():
    nope = np.array([h * 96 + d for h in range(4) for d in range(64)])
    rope = np.zeros(128, np.int64)
    for half in range(2):
        for h in range(4):
            for r in range(16):
                rope[half * 64 + h * 16 + r] = h * 96 + 64 + half * 16 + r
    return np.concatenate([nope, rope])


def _perm_w_ukv():
    knope = np.array([h * 128 + d for h in range(4) for d in range(64)])
    v = np.array([h * 128 + 64 + d for h in range(4) for d in range(64)])
    return np.concatenate([knope, v])


def _rope_tables(seq):
    def table(dim, reps):
        inv = 1.0 / (ROPE_THETA ** (jnp.arange(0, dim, 2, dtype=F32) / dim))
        ang = jnp.arange(seq, dtype=F32)[:, None] * inv[None, :]
        cos, sin = jnp.cos(ang), jnp.sin(ang)
        return (jnp.tile(cos, (1, 2 * reps)),
                jnp.concatenate([jnp.tile(-sin, (1, reps)), jnp.tile(sin, (1, reps))], axis=1))
    cos_a, sin_a = table(HEAD_DIM, 2)
    cos_d, sin_d = table(DIFF_QK_DIM, 4)
    return cos_a, sin_a, cos_d, sin_d


def _proj_kernel(x_ref, g_ref, w_ref, cosa_ref, sina_ref, cosd_ref, sind_ref, qn_ref, wuq_ref, kvn_ref,
                 wukv_ref, oa_ref, od_ref, oc_ref, qb_ref, kb_ref, vb_ref):
    h = _rms(x_ref[...], g_ref[...]).astype(BF16)
    proj = _dot(h, w_ref[...])
    cosa, sina = cosa_ref[...], sina_ref[...]
    cosd, sind = cosd_ref[...], sind_ref[...]

    def rope(t, cos, sin):
        return t * cos + pltpu.roll(t, 64, 1) * sin

    def grp(c):
        return proj[:, c * LANES:(c + 1) * LANES]

    for c in range(2):
        oa_ref[:, c * LANES:(c + 1) * LANES] = (rope(grp(c), cosa, sina) * SCALE_A).astype(BF16)
        oa_ref[:, (2 + c) * LANES:(3 + c) * LANES] = rope(grp(2 + c), cosa, sina).astype(BF16)
    oa_ref[:, 512:768] = proj[:, 512:768].astype(BF16)
    for c in range(2):
        od_ref[:, c * LANES:(c + 1) * LANES] = (rope(grp(6 + c), cosd, sind) * SCALE_D).astype(BF16)
        od_ref[:, (2 + c) * LANES:(3 + c) * LANES] = rope(grp(8 + c), cosd, sind).astype(BF16)
    od_ref[:, 512:768] = proj[:, 1280:1536].astype(BF16)
    oc_ref[:, 0:256] = (proj[:, 1536:1792] * SCALE_C).astype(BF16)
    oc_ref[:, 256:768] = proj[:, 1792:2304].astype(BF16)
    cq = _rms(proj[:, 2304:2560], qn_ref[...]).astype(BF16)
    q = _dot(cq, wuq_ref[...])
    q_rope = (rope(q[:, 256:384], cosd, sind) * SCALE_B).astype(BF16)
    ckv = _rms(proj[:, 2560:2688], kvn_ref[...]).astype(BF16)
    kv = _dot(ckv, wukv_ref[...])
    k_rope = rope(proj[:, 2688:2816], cosd, sind).astype(BF16)
    for p in range(2):
        qb_ref[:, p * 256:p * 256 + LANES] = (q[:, p * LANES:(p + 1) * LANES] * SCALE_B).astype(BF16)
        qb_ref[:, p * 256 + LANES:(p + 1) * 256] = q_rope
        kb_ref[:, p * 256:p * 256 + LANES] = kv[:, p * LANES:(p + 1) * LANES].astype(BF16)
        kb_ref[:, p * 256 + LANES:(p + 1) * 256] = k_rope
    vb_ref[...] = kv[:, 256:512].astype(BF16)


def _proj(x2d, g, w_in, tables, qn, wuq, kvn, wukv, seq, tm):
    t, d = x2d.shape
    nseq = seq // tm
    row = lambda i: (i, 0)
    const = lambda i: (0, 0)
    tab = pl.BlockSpec((tm, LANES), lambda i: (i % nseq, 0))
    outs = [(768,), (768,), (768,), (512,), (512,), (256,)]
    return pl.pallas_call(
        _proj_kernel,
        grid=(t // tm,),
        in_specs=[pl.BlockSpec((tm, d), row), pl.BlockSpec((1, d), const),
                  pl.BlockSpec((d, PROJ_COLS), const), tab, tab, tab, tab,
                  pl.BlockSpec((1, MLA_Q_RANK), const), pl.BlockSpec((MLA_Q_RANK, 384), const),
                  pl.BlockSpec((1, MLA_KV_RANK), const), pl.BlockSpec((MLA_KV_RANK, 512), const)],
        out_specs=[pl.BlockSpec((tm, c[0]), row) for c in outs],
        out_shape=[jax.ShapeDtypeStruct((t, c[0]), BF16) for c in outs],
        compiler_params=_cparams(("arbitrary",)),
        name="proj",
    )(x2d, g, w_in, *tables, qn, wuq, kvn, wukv)


def _tri_tables(nq, descending):
    qi, ki = [], []
    for q in range(nq):
        ks = range(q, -1, -1) if descending else range(q + 1)
        for k in ks:
            qi.append(q)
            ki.append(k)
    return jnp.asarray(qi, jnp.int32), jnp.asarray(ki, jnp.int32)


def _softmax_update(j, s, v, m_ref, l_ref, acc_ref):
    m_prev = m_ref[j]
    m_new = jnp.maximum(m_prev, jnp.max(s, axis=-1, keepdims=True))
    alpha = jnp.exp2(m_prev - m_new)
    p = jnp.exp2(s - m_new)
    l_ref[j] = alpha * l_ref[j] + jnp.sum(p, axis=-1, keepdims=True)
    acc_ref[j] = alpha * acc_ref[j] + _dot(p.astype(BF16), v)
    m_ref[j] = m_new


def _init_stats(m_ref, l_ref, acc_ref):
    m_ref[...] = jnp.full(m_ref.shape, NEG_INF, F32)
    l_ref[...] = jnp.zeros(l_ref.shape, F32)
    acc_ref[...] = jnp.zeros(acc_ref.shape, F32)


def _causal_mask(n):
    row = lax.broadcasted_iota(jnp.int32, (n, n), 0)
    col = lax.broadcasted_iota(jnp.int32, (n, n), 1)
    return row, col


def _mla_kernel(qi_ref, ki_ref, q_ref, k_ref, v_ref, o_ref, qm_ref, m_ref, l_ref, acc_ref):
    p = pl.program_id(1)
    t = pl.program_id(2)
    qi, ki = qi_ref[t], ki_ref[t]
    bq = q_ref.shape[0]

    @pl.when(ki == 0)
    def _():
        q = q_ref[...]
        lane = _lane(q.shape)
        for j in range(2):
            nope = (lane < LANES) & (lane // HEAD_DIM == j)
            rope = (lane >= LANES) & ((lane % 64) // 16 == 2 * p + j)
            qm_ref[j] = jnp.where(nope | rope, q, jnp.zeros_like(q))
        _init_stats(m_ref, l_ref, acc_ref)

    def step(masked):
        k, v = k_ref[...], v_ref[...]
        for j in range(2):
            s = _dot_nt(qm_ref[j], k)
            if masked:
                row, col = _causal_mask(bq)
                s = jnp.where(col <= row, s, NEG_INF)
            _softmax_update(j, s, v, m_ref, l_ref, acc_ref)

    pl.when(ki < qi)(lambda: step(False))

    @pl.when(ki == qi)
    def _():
        step(True)
        lane = _lane((bq, LANES))
        o = jnp.where(lane < HEAD_DIM, acc_ref[0] / l_ref[0], acc_ref[1] / l_ref[1])
        o_ref[...] = o.astype(o_ref.dtype)


def _mla_attention(qb, kb, vb, bq):
    b, s, _ = qb.shape
    nq = s // bq
    qi, ki = _tri_tables(nq, descending=False)
    grid_spec = pltpu.PrefetchScalarGridSpec(
        num_scalar_prefetch=2,
        grid=(b, 2, qi.shape[0]),
        in_specs=[pl.BlockSpec((None, bq, 256), lambda b_, p, t, qi, ki: (b_, qi[t], p)),
                  pl.BlockSpec((None, bq, 256), lambda b_, p, t, qi, ki: (b_, ki[t], p)),
                  pl.BlockSpec((None, bq, LANES), lambda b_, p, t, qi, ki: (b_, ki[t], p))],
        out_specs=pl.BlockSpec((None, bq, LANES), lambda b_, p, t, qi, ki: (b_, qi[t], p)),
        scratch_shapes=[pltpu.VMEM((2, bq, 256), BF16), pltpu.VMEM((2, bq, 1), F32),
                        pltpu.VMEM((2, bq, 1), F32), pltpu.VMEM((2, bq, LANES), F32)],
    )
    return pl.pallas_call(
        _mla_kernel, grid_spec=grid_spec,
        out_shape=jax.ShapeDtypeStruct((b, s, 256), BF16),
        compiler_params=_cparams(("arbitrary", "arbitrary", "arbitrary")),
        name="attn_mla",
    )(qi, ki, qb, kb, vb)


def _diff_kernel(qi_ref, ki_ref, q_ref, k_ref, v_ref, lam_ref, g_ref, o_ref, qm_ref, m_ref, l_ref, acc_ref,
                 *, lambda_init):
    t = pl.program_id(2)
    qi, ki = qi_ref[t], ki_ref[t]
    bq = q_ref.shape[0]

    @pl.when(ki == 0)
    def _():
        q = q_ref[...]
        lane = _lane(q.shape)
        for u in range(4):
            qm_ref[u] = jnp.where((lane % 64) // 16 == u, q, jnp.zeros_like(q))
        _init_stats(m_ref, l_ref, acc_ref)

    def step(masked):
        k, v = k_ref[...], v_ref[...]
        for u in range(4):
            s = _dot_nt(qm_ref[u], k)
            if masked:
                row, col = _causal_mask(bq)
                s = jnp.where(col <= row, s, NEG_INF)
            _softmax_update(u, s, v, m_ref, l_ref, acc_ref)

    pl.when(ki < qi)(lambda: step(False))

    @pl.when(ki == qi)
    def _():
        step(True)
        lf = lam_ref[...]
        lam = (jnp.exp(jnp.sum(lf[0:1] * lf[1:2], axis=-1, keepdims=True))
               - jnp.exp(jnp.sum(lf[2:3] * lf[3:4], axis=-1, keepdims=True)) + lambda_init)
        o0 = acc_ref[0] / l_ref[0] - lam * (acc_ref[1] / l_ref[1])
        o1 = acc_ref[2] / l_ref[2] - lam * (acc_ref[3] / l_ref[3])
        lane = _lane((bq, LANES))
        lo = lane < HEAD_DIM
        o = jnp.where(lo, o0, o1)
        sq = o * o
        ss = jnp.where(lo, jnp.sum(jnp.where(lo, sq, 0.0), axis=-1, keepdims=True),
                       jnp.sum(jnp.where(lo, 0.0, sq), axis=-1, keepdims=True))
        y = o * lax.rsqrt(ss * (1.0 / HEAD_DIM) + EPS) * g_ref[...]
        o_ref[...] = (y * (1.0 - lambda_init)).astype(o_ref.dtype)


def _diff_attention(qkv, diff_lambda, g128, lambda_init, bq):
    b, s, _ = qkv.shape
    nq = s // bq
    qi, ki = _tri_tables(nq, descending=False)
    grid_spec = pltpu.PrefetchScalarGridSpec(
        num_scalar_prefetch=2,
        grid=(b, 2, qi.shape[0]),
        in_specs=[pl.BlockSpec((None, bq, LANES), lambda b_, g, t, qi, ki: (b_, qi[t], g)),
                  pl.BlockSpec((None, bq, LANES), lambda b_, g, t, qi, ki: (b_, ki[t], 2 + g)),
                  pl.BlockSpec((None, bq, LANES), lambda b_, g, t, qi, ki: (b_, ki[t], 4 + g)),
                  pl.BlockSpec((4, DIFF_QK_DIM), lambda b_, g, t, qi, ki: (0, 0)),
                  pl.BlockSpec((1, LANES), lambda b_, g, t, qi, ki: (0, 0))],
        out_specs=pl.BlockSpec((None, bq, LANES), lambda b_, g, t, qi, ki: (b_, qi[t], g)),
        scratch_shapes=[pltpu.VMEM((4, bq, LANES), BF16), pltpu.VMEM((4, bq, 1), F32),
                        pltpu.VMEM((4, bq, 1), F32), pltpu.VMEM((4, bq, LANES), F32)],
    )
    return pl.pallas_call(
        functools.partial(_diff_kernel, lambda_init=lambda_init), grid_spec=grid_spec,
        out_shape=jax.ShapeDtypeStruct((b, s, 256), BF16),
        compiler_params=_cparams(("arbitrary", "arbitrary", "arbitrary")),
        name="attn_diff",
    )(qi, ki, qkv, qkv, qkv, diff_lambda, g128)


SB_CHUNK = 256


def _sb_kernel(qi_ref, ki_ref, q_ref, k_ref, v_ref, o_ref, qm_ref, carry_ref, acc_ref):
    t = pl.program_id(2)
    qi, ki = qi_ref[t], ki_ref[t]
    bq = q_ref.shape[0]
    bk = k_ref.shape[0]
    ck = min(SB_CHUNK, bk)

    @pl.when(ki == qi)
    def _():
        q = q_ref[...]
        lane = _lane(q.shape)
        for j in range(2):
            qm_ref[j] = jnp.where(lane // HEAD_DIM == j, q, jnp.zeros_like(q))
        carry_ref[...] = jnp.zeros(carry_ref.shape, F32)
        acc_ref[...] = jnp.zeros(acc_ref.shape, F32)

    def step(masked):
        r = lax.broadcasted_iota(jnp.int32, (ck, ck), 0)
        c = lax.broadcasted_iota(jnp.int32, (ck, ck), 1)
        tri = jnp.where(r >= c, 1.0, 0.0).astype(BF16)
        for j in range(2):
            for cidx in range(bk // ck - 1, -1, -1):
                k = k_ref[cidx * ck:(cidx + 1) * ck, :]
                v = v_ref[cidx * ck:(cidx + 1) * ck, :]
                z = _dot_nt(qm_ref[j], k)
                if masked:
                    row = lax.broadcasted_iota(jnp.int32, (bq, ck), 0)
                    col = lax.broadcasted_iota(jnp.int32, (bq, ck), 1) + cidx * ck
                    z = jnp.where(col < row, z, NEG_INF)
                lg = jnp.minimum(-z, 0.0) - jnp.log(1.0 + jnp.exp(-jnp.abs(z)))
                hi = lg.astype(BF16)
                lo = (lg - hi.astype(F32)).astype(BF16)
                carry = carry_ref[j]
                cum = _dot(hi, tri) + _dot(lo, tri) + carry
                a = jnp.exp(z + cum)
                acc_ref[j] += _dot(a.astype(BF16), v)
                carry_ref[j] = carry + jnp.sum(lg, axis=-1, keepdims=True)

    pl.when(ki == qi)(lambda: step(True))
    pl.when(ki < qi)(lambda: step(False))

    @pl.when(ki == 0)
    def _():
        lane = _lane((bq, LANES))
        o_ref[...] = jnp.where(lane < HEAD_DIM, acc_ref[0], acc_ref[1]).astype(o_ref.dtype)


def _sb_attention(qkv, bq):
    b, s, _ = qkv.shape
    nq = s // bq
    qi, ki = _tri_tables(nq, descending=True)
    grid_spec = pltpu.PrefetchScalarGridSpec(
        num_scalar_prefetch=2,
        grid=(b, 2, qi.shape[0]),
        in_specs=[pl.BlockSpec((None, bq, LANES), lambda b_, p, t, qi, ki: (b_, qi[t], p)),
                  pl.BlockSpec((None, bq, LANES), lambda b_, p, t, qi, ki: (b_, ki[t], 2 + p)),
                  pl.BlockSpec((None, bq, LANES), lambda b_, p, t, qi, ki: (b_, ki[t], 4 + p))],
        out_specs=pl.BlockSpec((None, bq, LANES), lambda b_, p, t, qi, ki: (b_, qi[t], p)),
        scratch_shapes=[pltpu.VMEM((2, bq, LANES), BF16), pltpu.VMEM((2, bq, 1), F32),
                        pltpu.VMEM((2, bq, LANES), F32)],
    )
    return pl.pallas_call(
        _sb_kernel, grid_spec=grid_spec,
        out_shape=jax.ShapeDtypeStruct((b, s, 256), BF16),
        compiler_params=_cparams(("arbitrary", "arbitrary", "arbitrary")),
        name="attn_sb",
    )(qi, ki, qkv, qkv, qkv)


DIL_REACH = max(w for w, _ in DIL_PATTERNS)


def _dil_kernel(q_ref, k_ref, v_ref, o_ref, bias_ref, qm_ref, m_ref, l_ref, acc_ref, *, n_off):
    first = (pl.program_id(0) == 0) & (pl.program_id(1) == 0) & (pl.program_id(2) == 0)
    qi = pl.program_id(2)
    off = pl.program_id(3)
    bq = q_ref.shape[0]

    @pl.when(first & (off == 0))
    def _():
        row, col = _causal_mask(bq)
        for o in range(n_off):
            delta = o * bq + row - col
            cnt = jnp.zeros((bq, bq), F32)
            for window, dil in DIL_PATTERNS:
                hit = (delta >= 0) & (delta <= window) & (delta % dil == 0)
                cnt = cnt + jnp.where(hit, 1.0, 0.0)
            bias_ref[o] = jnp.where(cnt > 0.5, jnp.log2(jnp.maximum(cnt, 1.0)), NEG_INF)

    @pl.when(off == 0)
    def _():
        q = q_ref[...]
        lane = _lane(q.shape)
        for j in range(2):
            qm_ref[j] = jnp.where((lane % 64) // 32 == j, q, jnp.zeros_like(q))
        _init_stats(m_ref, l_ref, acc_ref)

    @pl.when(off <= qi)
    def _():
        k, v = k_ref[...], v_ref[...]
        bias = bias_ref[off]
        for j in range(2):
            _softmax_update(j, _dot_nt(qm_ref[j], k) + bias, v, m_ref, l_ref, acc_ref)

    @pl.when(off == n_off - 1)
    def _():
        lane = _lane((bq, LANES))
        o = jnp.where(lane < HEAD_DIM, acc_ref[0] / l_ref[0], acc_ref[1] / l_ref[1])
        o_ref[...] = o.astype(o_ref.dtype)


def _dil_attention(qkv, bq):
    b, s, _ = qkv.shape
    nq = s // bq
    n_off = -(-DIL_REACH // bq) + 1
    kv_blk = lambda b_, p, qi, off: jnp.maximum(qi - off, 0)
    return pl.pallas_call(
        functools.partial(_dil_kernel, n_off=n_off),
        grid=(b, 2, nq, n_off),
        in_specs=[pl.BlockSpec((None, bq, LANES), lambda b_, p, qi, off: (b_, qi, p)),
                  pl.BlockSpec((None, bq, LANES), lambda b_, p, qi, off: (b_, kv_blk(b_, p, qi, off), 2 + p)),
                  pl.BlockSpec((None, bq, LANES), lambda b_, p, qi, off: (b_, kv_blk(b_, p, qi, off), 4 + p))],
        out_specs=pl.BlockSpec((None, bq, LANES), lambda b_, p, qi, off: (b_, qi, p)),
        out_shape=jax.ShapeDtypeStruct((b, s, 256), BF16),
        scratch_shapes=[pltpu.VMEM((n_off, bq, bq), F32), pltpu.VMEM((2, bq, LANES), BF16),
                        pltpu.VMEM((2, bq, 1), F32), pltpu.VMEM((2, bq, 1), F32),
                        pltpu.VMEM((2, bq, LANES), F32)],
        compiler_params=_cparams(("arbitrary", "arbitrary", "arbitrary", "arbitrary")),
        name="attn_dil",
    )(qkv, qkv, qkv)


def _memkv_kernel(mem_ref, g_ref, w_ref, k_ref, v_ref):
    h = _rms(mem_ref[...], g_ref[...]).astype(BF16)
    kv = _dot(h, w_ref[...])
    k_ref[...] = kv[:, 0:256].astype(BF16)
    v_ref[...] = kv[:, 256:512].astype(BF16)


def _memkv(mem2d, g, w):
    t, d = mem2d.shape
    tm = min(t, 256)
    return pl.pallas_call(
        _memkv_kernel, grid=(t // tm,),
        in_specs=[pl.BlockSpec((tm, d), lambda i: (i, 0)), pl.BlockSpec((1, d), lambda i: (0, 0)),
                  pl.BlockSpec((d, 512), lambda i: (0, 0))],
        out_specs=[pl.BlockSpec((tm, 256), lambda i: (i, 0))] * 2,
        out_shape=[jax.ShapeDtypeStruct((t, 256), BF16)] * 2,
        compiler_params=_cparams(("arbitrary",)),
        name="mem_kv",
    )(mem2d, g, w)


def _merge_kernel(x_ref, oa_ref, ob_ref, oc_ref, od_ref, g_ref, wg_ref, bg_ref, wb_ref, wo_ref, cg_ref,
                  wcq_ref, kc_ref, vc_ref, wco_ref, out_ref):
    x = x_ref[...]
    d = x.shape[1]
    h = _rms(x, g_ref[...]).astype(BF16)
    y = None
    for i, o_ref in enumerate((oa_ref, ob_ref, oc_ref, od_ref)):
        gate = _sigmoid(_dot(h, wg_ref[:, i * d:(i + 1) * d]) + bg_ref[:, i * d:(i + 1) * d])
        term = gate * _dot(o_ref[...], wb_ref[i])
        y = term if y is None else y + term
    x1 = x + _dot(y.astype(BF16), wo_ref[...])
    hc = _rms(x1, cg_ref[...]).astype(BF16)
    q = (_dot(hc, wcq_ref[...]) * (HEAD_DIM ** -0.5 * LOG2E)).astype(BF16)
    kc, vc = kc_ref[...], vc_ref[...]
    lane = _lane(q.shape)
    o = jnp.zeros(q.shape, F32)
    for hd in range(N_HEADS):
        sel = lane // HEAD_DIM == hd
        s = _dot_nt(jnp.where(sel, q, jnp.zeros_like(q)), kc)
        p = jnp.exp2(s - jnp.max(s, axis=-1, keepdims=True))
        oh = _dot(p.astype(BF16), vc) / jnp.sum(p, axis=-1, keepdims=True)
        o = jnp.where(sel, oh, o)
    out_ref[...] = x1 + _dot(o.astype(BF16), wco_ref[...])


def _merge(x2d, oa, ob, oc, od, g, wg, bg, wb, wo, cg, wcq, kc, vc, wco, seq, tm):
    t, d = x2d.shape
    nseq = seq // tm
    row = lambda i: (i, 0)
    c2 = lambda i: (0, 0)
    m = kc.shape[1]
    return pl.pallas_call(
        _merge_kernel, grid=(t // tm,),
        in_specs=[pl.BlockSpec((tm, d), row)] + [pl.BlockSpec((tm, 256), row)] * 4 + [
            pl.BlockSpec((1, d), c2), pl.BlockSpec((d, 4 * d), c2), pl.BlockSpec((1, 4 * d), c2),
            pl.BlockSpec((4, 256, d), lambda i: (0, 0, 0)), pl.BlockSpec((d, d), c2),
            pl.BlockSpec((1, d), c2), pl.BlockSpec((d, 256), c2),
            pl.BlockSpec((None, m, 256), lambda i: (i // nseq, 0, 0)),
            pl.BlockSpec((None, m, 256), lambda i: (i // nseq, 0, 0)),
            pl.BlockSpec((256, d), c2)],
        out_specs=pl.BlockSpec((tm, d), row),
        out_shape=jax.ShapeDtypeStruct((t, d), F32),
        compiler_params=_cparams(("arbitrary",)),
        name="merge_cross",
    )(x2d, oa, ob, oc, od, g, wg, bg, wb, wo, cg, wcq, kc, vc, wco)


FF_CHUNK = 512


def _swiglu_chunk(hf, w1, w3, w2):
    a = _dot(hf, w1)
    u = a * _sigmoid(a) * _dot(hf, w3)
    return _dot(u.astype(BF16), w2)


def _ffn_dense_kernel(x_ref, g_ref, w1_ref, w3_ref, w2_ref, fg_ref, out_ref, *, final):
    x = x_ref[...]
    hf = _rms(x, g_ref[...]).astype(BF16)
    y = x
    for c in range(w1_ref.shape[1] // FF_CHUNK):
        sl = slice(c * FF_CHUNK, (c + 1) * FF_CHUNK)
        y = y + _swiglu_chunk(hf, w1_ref[:, sl], w3_ref[:, sl], w2_ref[sl, :])
    out_ref[...] = _rms(y, fg_ref[...]) if final else y


def _ffn_dense(x2d, g, w1, w3, w2, fg, final, tm):
    t, d = x2d.shape
    dff = w1.shape[1]
    row = lambda i: (i, 0)
    c2 = lambda i: (0, 0)
    return pl.pallas_call(
        functools.partial(_ffn_dense_kernel, final=final), grid=(t // tm,),
        in_specs=[pl.BlockSpec((tm, d), row), pl.BlockSpec((1, d), c2), pl.BlockSpec((d, dff), c2),
                  pl.BlockSpec((d, dff), c2), pl.BlockSpec((dff, d), c2), pl.BlockSpec((1, d), c2)],
        out_specs=pl.BlockSpec((tm, d), row),
        out_shape=jax.ShapeDtypeStruct((t, d), F32),
        compiler_params=_cparams(("arbitrary",)),
        name="ffn_dense",
    )(x2d, g, w1, w3, w2, fg)


def _ffn_moe_kernel(x_ref, g_ref, wr_ref, w1_ref, w3_ref, w2_ref, fg_ref, out_ref, hf_ref, comb_ref, acc_ref,
                    *, final):
    e = pl.program_id(1)

    @pl.when(e == 0)
    def _():
        hf = _rms(x_ref[...], g_ref[...])
        hf_ref[...] = hf.astype(BF16)
        logits = jnp.dot(hf, wr_ref[...], preferred_element_type=F32, precision=lax.Precision.HIGHEST)
        lane = _lane(logits.shape)
        lanef = lane.astype(F32)
        lg = jnp.where(lane < N_EXPERTS, logits, -jnp.inf)
        m1 = jnp.max(lg, axis=-1, keepdims=True)
        i1 = jnp.min(jnp.where(lg == m1, lanef, float(LANES)), axis=-1, keepdims=True)
        lg2 = jnp.where(lanef == i1, -jnp.inf, lg)
        m2 = jnp.max(lg2, axis=-1, keepdims=True)
        i2 = jnp.min(jnp.where(lg2 == m2, lanef, float(LANES)), axis=-1, keepdims=True)
        e2 = jnp.exp(m2 - m1)
        den = 1.0 + e2
        comb_ref[...] = jnp.where(lanef == i1, 1.0 / den, 0.0) + jnp.where(lanef == i2, e2 / den, 0.0)
        acc_ref[...] = jnp.zeros(acc_ref.shape, F32)

    comb = comb_ref[...]
    ce = jnp.sum(jnp.where(_lane(comb.shape) == e, comb, 0.0), axis=-1, keepdims=True)
    acc_ref[...] += ce * _swiglu_chunk(hf_ref[...], w1_ref[...], w3_ref[...], w2_ref[...])

    @pl.when(e == N_EXPERTS - 1)
    def _():
        y = x_ref[...] + acc_ref[...]
        out_ref[...] = _rms(y, fg_ref[...]) if final else y


def _ffn_moe(x2d, g, wr, w1, w3, w2, fg, final, tm):
    t, d = x2d.shape
    row = lambda i, e: (i, 0)
    c2 = lambda i, e: (0, 0)
    return pl.pallas_call(
        functools.partial(_ffn_moe_kernel, final=final), grid=(t // tm, N_EXPERTS),
        in_specs=[pl.BlockSpec((tm, d), row), pl.BlockSpec((1, d), c2), pl.BlockSpec((d, LANES), c2),
                  pl.BlockSpec((None, d, D_FF_EXPERT), lambda i, e: (e, 0, 0)),
                  pl.BlockSpec((None, d, D_FF_EXPERT), lambda i, e: (e, 0, 0)),
                  pl.BlockSpec((None, D_FF_EXPERT, d), lambda i, e: (e, 0, 0)),
                  pl.BlockSpec((1, d), c2)],
        out_specs=pl.BlockSpec((tm, d), row),
        out_shape=jax.ShapeDtypeStruct((t, d), F32),
        scratch_shapes=[pltpu.VMEM((tm, d), BF16), pltpu.VMEM((tm, LANES), F32), pltpu.VMEM((tm, d), F32)],
        compiler_params=_cparams(("arbitrary", "arbitrary")),
        name="ffn_moe",
    )(x2d, g, wr, w1, w3, w2, fg)


def _tile(n, want):
    t = min(n, want)
    assert n % t == 0
    return t


def kernel(x, mem, mix_norm, w_in, mla_q_norm, mla_w_uq, mla_kv_norm, mla_w_ukv, diff_lambda, diff_norm, w_branch, w_gate, b_gate, w_out, cross_norm, mem_norm, w_cq, w_ckv, w_co, ffn_norm, dense_w1, dense_w3, dense_w2, moe_router, moe_w1, moe_w3, moe_w2, final_norm):
    b, s, d = x.shape
    depth = mix_norm.shape[0]
    t = b * s
    tm = _tile(s, 512)
    bq = _tile(s, 512)
    tables = _rope_tables(s)
    cols_in, cols_uq, cols_ukv = _perm_w_in(), _perm_w_uq(), _perm_w_ukv()
    row = lambda v: v.reshape(1, -1)
    fg = row(final_norm)

    xf = x.reshape(t, d)
    mem2d = mem.reshape(b * mem.shape[1], d)
    for layer in range(depth):
        lambda_init = 0.8 - 0.6 * math.exp(-0.3 * layer)
        final = layer == depth - 1
        oa, od, oc, qb, kb, vb = _proj(
            xf, row(mix_norm[layer]), w_in[layer][:, cols_in].astype(BF16), tables,
            row(mla_q_norm[layer]), mla_w_uq[layer][:, cols_uq].astype(BF16),
            row(mla_kv_norm[layer]), mla_w_ukv[layer][:, cols_ukv].astype(BF16), s, tm)
        shp = lambda a: a.reshape(b, s, a.shape[-1])
        ya = _dil_attention(shp(oa), bq)
        yb = _mla_attention(shp(qb), shp(kb), shp(vb), bq)
        yc = _sb_attention(shp(oc), bq)
        yd = _diff_attention(shp(od), diff_lambda[layer], row(jnp.tile(diff_norm[layer], 2)), lambda_init, bq)
        kc, vc = _memkv(mem2d, row(mem_norm[layer]), w_ckv[layer].astype(BF16))
        m = mem.shape[1]
        flat = lambda a: a.reshape(t, a.shape[-1])
        xf = _merge(xf, flat(ya), flat(yb), flat(yc), flat(yd), row(mix_norm[layer]),
                    w_gate[layer].astype(BF16), row(b_gate[layer]), w_branch[layer].astype(BF16),
                    w_out[layer].astype(BF16), row(cross_norm[layer]), w_cq[layer].astype(BF16),
                    kc.reshape(b, m, 256), vc.reshape(b, m, 256), w_co[layer].astype(BF16), s, tm)
        i = layer // 2
        if layer % 2 == 0:
            xf = _ffn_dense(xf, row(ffn_norm[layer]), dense_w1[i].astype(BF16), dense_w3[i].astype(BF16),
                            dense_w2[i].astype(BF16), fg, final, tm)
        else:
            wr = jnp.pad(moe_router[i], ((0, 0), (0, LANES - N_EXPERTS)))
            xf = _ffn_moe(xf, row(ffn_norm[layer]), wr, moe_w1[i].astype(BF16), moe_w3[i].astype(BF16),
                          moe_w2[i].astype(BF16), fg, final, _tile(t, 1024))
    return xf.reshape(b, s, d)
```

```python
import functools
import math

import numpy as np
import jax
import jax.numpy as jnp
from jax import lax
from jax.experimental import pallas as pl
from jax.experimental.pallas import tpu as pltpu

F32 = jnp.float32
BF16 = jnp.bfloat16

HEAD_DIM = 64
ROPE_THETA = 10000.0
NEG_INF = -1e30
EPS = 1e-6
LOG2E = math.log2(math.e)

DIL_PATTERNS = ((128, 1), (512, 4), (2048, 16))
N_HEADS = 4
MLA_Q_RANK = 256
MLA_KV_RANK = 128
MLA_NOPE_DIM = 64
MLA_ROPE_DIM = 32
DIFF_QK_DIM = 32
N_EXPERTS = 8
D_FF_EXPERT = 512
LANES = 128

VMEM_LIMIT = 56 * 1024 * 1024


def _cparams(sem):
    return pltpu.CompilerParams(dimension_semantics=sem, vmem_limit_bytes=VMEM_LIMIT)


def _rms(x, g):
    return x * lax.rsqrt(jnp.mean(x * x, axis=-1, keepdims=True) + EPS) * g


def _dot(a, b):
    return jnp.dot(a, b, preferred_element_type=F32)


def _dot_nt(a, b):
    return lax.dot_general(a, b, (((1,), (1,)), ((), ())), preferred_element_type=F32)


def _sigmoid(x):
    return 1.0 / (1.0 + jnp.exp(-x))


def _lane(shape):
    return lax.broadcasted_iota(jnp.int32, shape, len(shape) - 1)


PROJ_COLS = 2816
SCALE_A = HEAD_DIM ** -0.5 * LOG2E
SCALE_B = (MLA_NOPE_DIM + MLA_ROPE_DIM) ** -0.5 * LOG2E
SCALE_C = HEAD_DIM ** -0.5 * LOG2E
SCALE_D = DIFF_QK_DIM ** -0.5 * LOG2E


def _perm_w_in():
    a_off, b_off, c_off, d_off = 0, 768, 1184, 1952
    pair = np.zeros(256, np.int64)
    for p in range(2):
        for half in range(2):
            for j in range(2):
                for r in range(32):
                    pair[p * 128 + half * 64 + j * 32 + r] = (2 * p + j) * 64 + half * 32 + r
    unit = np.zeros(256, np.int64)
    for g in range(2):
        for half in range(2):
            for u in range(4):
                for r in range(16):
                    unit[g * 128 + half * 64 + u * 16 + r] = (4 * g + u) * 32 + half * 16 + r
    nat = np.arange(256)
    krope = np.concatenate([np.tile(np.arange(16), 4), np.tile(16 + np.arange(16), 4)])
    cols = np.concatenate([
        a_off + pair, a_off + 256 + pair, a_off + 512 + nat,
        d_off + unit, d_off + 256 + unit, d_off + 512 + nat,
        c_off + np.arange(768),
        b_off + np.arange(MLA_Q_RANK + MLA_KV_RANK),
        b_off + MLA_Q_RANK + MLA_KV_RANK + krope,
    ])
    assert cols.shape[0] == PROJ_COLS
    return cols


def _perm_w_uq():
    nope = np.array([h * 96 + d for h in range(4) for d in range(64)])
    rope = np.zeros(128, np.int64)
    for half in range(2):
        for h in range(4):
            for r in range(16):
                rope[half * 64 + h * 16 + r] = h * 96 + 64 + half * 16 + r
    return np.concatenate([nope, rope])


def _perm_w_ukv():
    knope = np.array([h * 128 + d for h in range(4) for d in range(64)])
    v = np.array([h * 128 + 64 + d for h in range(4) for d in range(64)])
    return np.concatenate([knope, v])


def _rope_tables(seq):
    def table(dim, reps):
        inv = 1.0 / (ROPE_THETA ** (jnp.arange(0, dim, 2, dtype=F32) / dim))
        ang = jnp.arange(seq, dtype=F32)[:, None] * inv[None, :]
        cos, sin = jnp.cos(ang), jnp.sin(ang)
        return (jnp.tile(cos, (1, 2 * reps)),
                jnp.concatenate([jnp.tile(-sin, (1, reps)), jnp.tile(sin, (1, reps))], axis=1))
    cos_a, sin_a = table(HEAD_DIM, 2)
    cos_d, sin_d = table(DIFF_QK_DIM, 4)
    return cos_a, sin_a, cos_d, sin_d


def _proj_kernel(x_ref, g_ref, w_ref, cosa_ref, sina_ref, cosd_ref, sind_ref, qn_ref, wuq_ref, kvn_ref,
                 wukv_ref, qta_ref, ka_ref, vta_ref, qtd_ref, kd_ref, vtd_ref, qtc_ref, kc_ref, vtc_ref,
                 qtb_ref, kb_ref, vtb_ref):
    h = _rms(x_ref[...], g_ref[...]).astype(BF16)
    proj = _dot(h, w_ref[...])
    cosa, sina = cosa_ref[...], sina_ref[...]
    cosd, sind = cosd_ref[...], sind_ref[...]

    def rope(t, cos, sin):
        return t * cos + pltpu.roll(t, 64, 1) * sin

    def grp(c):
        return proj[:, c * LANES:(c + 1) * LANES]

    def put_t(ref, c, val):
        ref[c * LANES:(c + 1) * LANES, :] = val.T.astype(BF16)

    for c in range(2):
        put_t(qta_ref, c, rope(grp(c), cosa, sina) * SCALE_A)
        ka_ref[:, c * LANES:(c + 1) * LANES] = rope(grp(2 + c), cosa, sina).astype(BF16)
        put_t(vta_ref, c, grp(4 + c))
        put_t(qtd_ref, c, rope(grp(6 + c), cosd, sind) * SCALE_D)
        kd_ref[:, c * LANES:(c + 1) * LANES] = rope(grp(8 + c), cosd, sind).astype(BF16)
        put_t(vtd_ref, c, grp(10 + c))
        put_t(qtc_ref, c, grp(12 + c) * SCALE_C)
        put_t(vtc_ref, c, grp(16 + c))
    kc_ref[...] = proj[:, 1792:2048].astype(BF16)
    cq = _rms(proj[:, 2304:2560], qn_ref[...]).astype(BF16)
    q = _dot(cq, wuq_ref[...])
    q_rope = rope(q[:, 256:384], cosd, sind) * SCALE_B
    ckv = _rms(proj[:, 2560:2688], kvn_ref[...]).astype(BF16)
    kv = _dot(ckv, wukv_ref[...])
    k_rope = rope(proj[:, 2688:2816], cosd, sind).astype(BF16)
    for p in range(2):
        put_t(qtb_ref, 2 * p, q[:, p * LANES:(p + 1) * LANES] * SCALE_B)
        put_t(qtb_ref, 2 * p + 1, q_rope)
        kb_ref[:, p * 256:p * 256 + LANES] = kv[:, p * LANES:(p + 1) * LANES].astype(BF16)
        kb_ref[:, p * 256 + LANES:(p + 1) * 256] = k_rope
        put_t(vtb_ref, p, kv[:, 256 + p * LANES:256 + (p + 1) * LANES])


def _proj(x2d, g, w_in, tables, qn, wuq, kvn, wukv, batch, seq, tm, bk, bk_a):
    t, d = x2d.shape
    nseq = seq // tm
    row = lambda i: (i, 0)
    const = lambda i: (0, 0)
    tab = pl.BlockSpec((tm, LANES), lambda i: (i % nseq, 0))

    def qt(c):
        return (pl.BlockSpec((None, c, tm), lambda i: (i // nseq, 0, i % nseq)),
                jax.ShapeDtypeStruct((batch, c, seq), BF16))

    def kr(c):
        return pl.BlockSpec((tm, c), row), jax.ShapeDtypeStruct((t, c), BF16)

    def vt(blk):
        per = blk // tm
        return (pl.BlockSpec((None, None, 256, tm), lambda i: (i // nseq, (i % nseq) // per, 0, (i % nseq) % per)),
                jax.ShapeDtypeStruct((batch, seq // blk, 256, blk), BF16))

    outs = [qt(256), kr(256), vt(bk_a), qt(256), kr(256), vt(bk), qt(256), kr(256), vt(bk),
            qt(512), kr(512), vt(bk)]
    return pl.pallas_call(
        _proj_kernel,
        grid=(t // tm,),
        in_specs=[pl.BlockSpec((tm, d), row), pl.BlockSpec((1, d), const),
                  pl.BlockSpec((d, PROJ_COLS), const), tab, tab, tab, tab,
                  pl.BlockSpec((1, MLA_Q_RANK), const), pl.BlockSpec((MLA_Q_RANK, 384), const),
                  pl.BlockSpec((1, MLA_KV_RANK), const), pl.BlockSpec((MLA_KV_RANK, 512), const)],
        out_specs=[o[0] for o in outs],
        out_shape=[o[1] for o in outs],
        compiler_params=_cparams(("arbitrary",)),
        name="proj",
    )(x2d, g, w_in, *tables, qn, wuq, kvn, wukv)


def _softmax_update(j, s, vt, m_ref, l_ref, acc_ref):
    m_prev = m_ref[j]
    m_new = jnp.maximum(m_prev, jnp.max(s, axis=0, keepdims=True))
    alpha = jnp.exp2(m_prev - m_new)
    p = jnp.exp2(s - m_new)
    l_ref[j] = alpha * l_ref[j] + jnp.sum(p, axis=0, keepdims=True)
    acc_ref[j] = alpha * acc_ref[j] + _dot(vt, p.astype(BF16))
    m_ref[j] = m_new


def _init_stats(m_ref, l_ref, acc_ref):
    m_ref[...] = jnp.full(m_ref.shape, NEG_INF, F32)
    l_ref[...] = jnp.zeros(l_ref.shape, F32)
    acc_ref[...] = jnp.zeros(acc_ref.shape, F32)


def _mask_rows(qt_ref, qm_ref, row_sel):
    qt = qt_ref[...]
    r = lax.broadcasted_iota(jnp.int32, qt.shape, 0)
    for j in range(qm_ref.shape[0]):
        qm_ref[j] = jnp.where(row_sel(r, j), qt, jnp.zeros_like(qt))


def _key_query_pos(k0, q0, bk, bq):
    kpos = k0 + lax.broadcasted_iota(jnp.int32, (bk, bq), 0)
    qpos = q0 + lax.broadcasted_iota(jnp.int32, (bk, bq), 1)
    return kpos, qpos


def _causal_softmax_sweep(qi, k_ref, vt_ref, qm_ref, m_ref, l_ref, acc_ref, vhead, bk):
    bq = qm_ref.shape[2]
    nmaps = qm_ref.shape[0]

    def block(kb, masked):
        k0 = pl.multiple_of(kb * bk, bk)
        k = k_ref[pl.ds(k0, bk), :]
        for j in range(nmaps):
            s = _dot(k, qm_ref[j])
            if masked:
                kpos, qpos = _key_query_pos(k0, qi * bq, bk, bq)
                s = jnp.where(kpos <= qpos, s, NEG_INF)
            hv = vhead(j)
            _softmax_update(j, s, vt_ref[kb, hv * HEAD_DIM:(hv + 1) * HEAD_DIM, :], m_ref, l_ref, acc_ref)

    n_full = (qi * bq) // bk

    def body(kb, carry):
        block(kb, False)
        return carry

    lax.fori_loop(0, n_full, body, 0)
    block(n_full, True)


def _attn_specs(batch, seq, qrows, kcols, bq, bk):
    nkb = seq // bk
    return dict(
        grid=(batch, 2, seq // bq),
        in_specs=[pl.BlockSpec((None, qrows, bq), lambda b, p, qi: (b, p, qi)),
                  pl.BlockSpec((None, seq, kcols), lambda b, p, qi: (b, 0, p)),
                  pl.BlockSpec((None, nkb, LANES, bk), lambda b, p, qi: (b, 0, p, 0))],
        out_specs=pl.BlockSpec((None, bq, LANES), lambda b, p, qi: (b, qi, p)),
        out_shape=jax.ShapeDtypeStruct((batch, seq, 256), BF16),
        compiler_params=_cparams(("arbitrary", "arbitrary", "arbitrary")),
    )


def _mla_kernel(qt_ref, k_ref, vt_ref, o_ref, qm_ref, m_ref, l_ref, acc_ref, *, bk):
    p = pl.program_id(1)
    qi = pl.program_id(2)

    def sel(r, j):
        nope = (r < LANES) & (r // HEAD_DIM == j)
        rope = (r >= LANES) & ((r % 64) // 16 == 2 * p + j)
        return nope | rope

    _mask_rows(qt_ref, qm_ref, sel)
    _init_stats(m_ref, l_ref, acc_ref)
    _causal_softmax_sweep(qi, k_ref, vt_ref, qm_ref, m_ref, l_ref, acc_ref, lambda j: j, bk)
    o_t = jnp.concatenate([acc_ref[0] / l_ref[0], acc_ref[1] / l_ref[1]], axis=0)
    o_ref[...] = o_t.T.astype(o_ref.dtype)


def _mla_attention(qt, k, vt, bq, bk):
    b, _, s = qt.shape
    spec = _attn_specs(b, s, 256, 256, bq, bk)
    return pl.pallas_call(
        functools.partial(_mla_kernel, bk=bk), **spec,
        scratch_shapes=[pltpu.VMEM((2, 256, bq), BF16), pltpu.VMEM((2, 1, bq), F32),
                        pltpu.VMEM((2, 1, bq), F32), pltpu.VMEM((2, HEAD_DIM, bq), F32)],
        name="attn_mla",
    )(qt, k, vt)


def _diff_kernel(qt_ref, k_ref, vt_ref, lam_ref, g_ref, o_ref, qm_ref, m_ref, l_ref, acc_ref, *, bk, lambda_init):
    qi = pl.program_id(2)
    _mask_rows(qt_ref, qm_ref, lambda r, u: (r % 64) // 16 == u)
    _init_stats(m_ref, l_ref, acc_ref)
    _causal_softmax_sweep(qi, k_ref, vt_ref, qm_ref, m_ref, l_ref, acc_ref, lambda u: u // 2, bk)
    lf = lam_ref[...]
    lam = (jnp.exp(jnp.sum(lf[0:1] * lf[1:2], axis=-1, keepdims=True))
           - jnp.exp(jnp.sum(lf[2:3] * lf[3:4], axis=-1, keepdims=True)) + lambda_init)
    heads = []
    for j in range(2):
        o = acc_ref[2 * j] / l_ref[2 * j] - lam * (acc_ref[2 * j + 1] / l_ref[2 * j + 1])
        heads.append(o * lax.rsqrt(jnp.mean(o * o, axis=0, keepdims=True) + EPS))
    y = jnp.concatenate(heads, axis=0).T * g_ref[...]
    o_ref[...] = (y * (1.0 - lambda_init)).astype(o_ref.dtype)


def _diff_attention(qt, k, vt, diff_lambda, g128, lambda_init, bq, bk):
    b, _, s = qt.shape
    spec = _attn_specs(b, s, LANES, LANES, bq, bk)
    spec["in_specs"] = spec["in_specs"] + [pl.BlockSpec((4, DIFF_QK_DIM), lambda b_, g, qi: (0, 0)),
                                           pl.BlockSpec((1, LANES), lambda b_, g, qi: (0, 0))]
    return pl.pallas_call(
        functools.partial(_diff_kernel, bk=bk, lambda_init=lambda_init), **spec,
        scratch_shapes=[pltpu.VMEM((4, LANES, bq), BF16), pltpu.VMEM((4, 1, bq), F32),
                        pltpu.VMEM((4, 1, bq), F32), pltpu.VMEM((4, HEAD_DIM, bq), F32)],
        name="attn_diff",
    )(qt, k, vt, diff_lambda, g128)


SB_CHUNK = 256


def _sb_kernel(qt_ref, k_ref, vt_ref, o_ref, qm_ref, carry_ref, acc_ref, *, bk):
    qi = pl.program_id(2)
    bq = qt_ref.shape[1]
    ck = min(SB_CHUNK, bk)
    _mask_rows(qt_ref, qm_ref, lambda r, j: r // HEAD_DIM == j)
    carry_ref[...] = jnp.zeros(carry_ref.shape, F32)
    acc_ref[...] = jnp.zeros(acc_ref.shape, F32)
    rr = lax.broadcasted_iota(jnp.int32, (ck, ck), 0)
    cc = lax.broadcasted_iota(jnp.int32, (ck, ck), 1)
    suffix = jnp.where(cc >= rr, 1.0, 0.0).astype(BF16)

    def block(kb, masked):
        k0 = pl.multiple_of(kb * bk, bk)
        k = k_ref[pl.ds(k0, bk), :]
        for j in range(2):
            z = _dot(k, qm_ref[j])
            if masked:
                kpos, qpos = _key_query_pos(k0, qi * bq, bk, bq)
                z = jnp.where(kpos < qpos, z, NEG_INF)
            for c in range(bk // ck - 1, -1, -1):
                zc = z[c * ck:(c + 1) * ck]
                nz = -zc
                lg = jnp.minimum(nz, 0.0) - jnp.log2(1.0 + jnp.exp2(jnp.minimum(zc, nz)))
                hi = lg.astype(BF16)
                lo = (lg - hi.astype(F32)).astype(BF16)
                carry = carry_ref[j]
                cum = _dot(suffix, hi) + _dot(suffix, lo) + carry
                a = jnp.exp2(zc + cum)
                vt = vt_ref[kb, j * HEAD_DIM:(j + 1) * HEAD_DIM, c * ck:(c + 1) * ck]
                acc_ref[j] += _dot(vt, a.astype(BF16))
                carry_ref[j] = carry + jnp.sum(lg, axis=0, keepdims=True)

    n_full = (qi * bq) // bk
    block(n_full, True)

    def body(i, carry):
        block(n_full - 1 - i, False)
        return carry

    lax.fori_loop(0, n_full, body, 0)
    o_ref[...] = jnp.concatenate([acc_ref[0], acc_ref[1]], axis=0).T.astype(o_ref.dtype)


def _sb_attention(qt, k, vt, bq, bk):
    b, _, s = qt.shape
    spec = _attn_specs(b, s, LANES, LANES, bq, bk)
    return pl.pallas_call(
        functools.partial(_sb_kernel, bk=bk), **spec,
        scratch_shapes=[pltpu.VMEM((2, LANES, bq), BF16), pltpu.VMEM((2, 1, bq), F32),
                        pltpu.VMEM((2, HEAD_DIM, bq), F32)],
        name="attn_sb",
    )(qt, k, vt)


DIL_REACH = max(w for w, _ in DIL_PATTERNS)


def _dil_kernel(qt_ref, k_ref, vt_ref, o_ref, bias_ref, qm_ref, m_ref, l_ref, acc_ref, *, n_off):
    first = (pl.program_id(0) == 0) & (pl.program_id(1) == 0) & (pl.program_id(2) == 0)
    qi = pl.program_id(2)
    bq = qt_ref.shape[1]

    @pl.when(first)
    def _():
        kk = lax.broadcasted_iota(jnp.int32, (bq, bq), 0)
        qq = lax.broadcasted_iota(jnp.int32, (bq, bq), 1)
        for o in range(n_off):
            delta = o * bq + qq - kk
            cnt = jnp.zeros((bq, bq), F32)
            for window, dil in DIL_PATTERNS:
                hit = (delta >= 0) & (delta <= window) & (delta % dil == 0)
                cnt = cnt + jnp.where(hit, 1.0, 0.0)
            bias_ref[o] = jnp.where(cnt > 0.5, jnp.log2(jnp.maximum(cnt, 1.0)), NEG_INF)

    _mask_rows(qt_ref, qm_ref, lambda r, j: (r % 64) // 32 == j)
    _init_stats(m_ref, l_ref, acc_ref)

    def body(o, carry):
        kb = qi - o
        k = k_ref[pl.ds(pl.multiple_of(kb * bq, bq), bq), :]
        bias = bias_ref[o]
        for j in range(2):
            s = _dot(k, qm_ref[j]) + bias
            _softmax_update(j, s, vt_ref[kb, j * HEAD_DIM:(j + 1) * HEAD_DIM, :], m_ref, l_ref, acc_ref)
        return carry

    lax.fori_loop(0, jnp.minimum(qi, n_off - 1) + 1, body, 0)
    o_t = jnp.concatenate([acc_ref[0] / l_ref[0], acc_ref[1] / l_ref[1]], axis=0)
    o_ref[...] = o_t.T.astype(o_ref.dtype)


def _dil_attention(qt, k, vt, bq):
    b, _, s = qt.shape
    n_off = -(-DIL_REACH // bq) + 1
    spec = _attn_specs(b, s, LANES, LANES, bq, bq)
    return pl.pallas_call(
        functools.partial(_dil_kernel, n_off=n_off), **spec,
        scratch_shapes=[pltpu.VMEM((n_off, bq, bq), F32), pltpu.VMEM((2, LANES, bq), BF16),
                        pltpu.VMEM((2, 1, bq), F32), pltpu.VMEM((2, 1, bq), F32),
                        pltpu.VMEM((2, HEAD_DIM, bq), F32)],
        name="attn_dil",
    )(qt, k, vt)


def _memkv_kernel(mem_ref, g_ref, w_ref, k_ref, v_ref):
    h = _rms(mem_ref[...], g_ref[...]).astype(BF16)
    kv = _dot(h, w_ref[...])
    k_ref[...] = kv[:, 0:256].astype(BF16)
    v_ref[...] = kv[:, 256:512].astype(BF16)


def _memkv(mem2d, g, w):
    t, d = mem2d.shape
    tm = min(t, 256)
    return pl.pallas_call(
        _memkv_kernel, grid=(t // tm,),
        in_specs=[pl.BlockSpec((tm, d), lambda i: (i, 0)), pl.BlockSpec((1, d), lambda i: (0, 0)),
                  pl.BlockSpec((d, 512), lambda i: (0, 0))],
        out_specs=[pl.BlockSpec((tm, 256), lambda i: (i, 0))] * 2,
        out_shape=[jax.ShapeDtypeStruct((t, 256), BF16)] * 2,
        compiler_params=_cparams(("arbitrary",)),
        name="mem_kv",
    )(mem2d, g, w)


def _merge_kernel(x_ref, oa_ref, ob_ref, oc_ref, od_ref, g_ref, wg_ref, bg_ref, wb_ref, wo_ref, cg_ref,
                  wcq_ref, kc_ref, vc_ref, wco_ref, out_ref):
    x = x_ref[...]
    d = x.shape[1]
    h = _rms(x, g_ref[...]).astype(BF16)
    y = None
    for i, o_ref in enumerate((oa_ref, ob_ref, oc_ref, od_ref)):
        gate = _sigmoid(_dot(h, wg_ref[:, i * d:(i + 1) * d]) + bg_ref[:, i * d:(i + 1) * d])
        term = gate * _dot(o_ref[...], wb_ref[i])
        y = term if y is None else y + term
    x1 = x + _dot(y.astype(BF16), wo_ref[...])
    hc = _rms(x1, cg_ref[...]).astype(BF16)
    q = (_dot(hc, wcq_ref[...]) * (HEAD_DIM ** -0.5 * LOG2E)).astype(BF16)
    kc, vc = kc_ref[...], vc_ref[...]
    lane = _lane(q.shape)
    o = jnp.zeros(q.shape, F32)
    for hd in range(N_HEADS):
        sel = lane // HEAD_DIM == hd
        s = _dot_nt(jnp.where(sel, q, jnp.zeros_like(q)), kc)
        p = jnp.exp2(s - jnp.max(s, axis=-1, keepdims=True))
        oh = _dot(p.astype(BF16), vc) / jnp.sum(p, axis=-1, keepdims=True)
        o = jnp.where(sel, oh, o)
    out_ref[...] = x1 + _dot(o.astype(BF16), wco_ref[...])


def _merge(x2d, oa, ob, oc, od, g, wg, bg, wb, wo, cg, wcq, kc, vc, wco, seq, tm):
    t, d = x2d.shape
    nseq = seq // tm
    row = lambda i: (i, 0)
    c2 = lambda i: (0, 0)
    m = kc.shape[1]
    return pl.pallas_call(
        _merge_kernel, grid=(t // tm,),
        in_specs=[pl.BlockSpec((tm, d), row)] + [pl.BlockSpec((tm, 256), row)] * 4 + [
            pl.BlockSpec((1, d), c2), pl.BlockSpec((d, 4 * d), c2), pl.BlockSpec((1, 4 * d), c2),
            pl.BlockSpec((4, 256, d), lambda i: (0, 0, 0)), pl.BlockSpec((d, d), c2),
            pl.BlockSpec((1, d), c2), pl.BlockSpec((d, 256), c2),
            pl.BlockSpec((None, m, 256), lambda i: (i // nseq, 0, 0)),
            pl.BlockSpec((None, m, 256), lambda i: (i // nseq, 0, 0)),
            pl.BlockSpec((256, d), c2)],
        out_specs=pl.BlockSpec((tm, d), row),
        out_shape=jax.ShapeDtypeStruct((t, d), F32),
        compiler_params=_cparams(("arbitrary",)),
        name="merge_cross",
    )(x2d, oa, ob, oc, od, g, wg, bg, wb, wo, cg, wcq, kc, vc, wco)


FF_CHUNK = 512


def _swiglu_chunk(hf, w1, w3, w2):
    a = _dot(hf, w1)
    u = a * _sigmoid(a) * _dot(hf, w3)
    return _dot(u.astype(BF16), w2)


def _ffn_dense_kernel(x_ref, g_ref, w1_ref, w3_ref, w2_ref, fg_ref, out_ref, *, final):
    x = x_ref[...]
    hf = _rms(x, g_ref[...]).astype(BF16)
    y = x
    for c in range(w1_ref.shape[1] // FF_CHUNK):
        sl = slice(c * FF_CHUNK, (c + 1) * FF_CHUNK)
        y = y + _swiglu_chunk(hf, w1_ref[:, sl], w3_ref[:, sl], w2_ref[sl, :])
    out_ref[...] = _rms(y, fg_ref[...]) if final else y


def _ffn_dense(x2d, g, w1, w3, w2, fg, final, tm):
    t, d = x2d.shape
    dff = w1.shape[1]
    row = lambda i: (i, 0)
    c2 = lambda i: (0, 0)
    return pl.pallas_call(
        functools.partial(_ffn_dense_kernel, final=final), grid=(t // tm,),
        in_specs=[pl.BlockSpec((tm, d), row), pl.BlockSpec((1, d), c2), pl.BlockSpec((d, dff), c2),
                  pl.BlockSpec((d, dff), c2), pl.BlockSpec((dff, d), c2), pl.BlockSpec((1, d), c2)],
        out_specs=pl.BlockSpec((tm, d), row),
        out_shape=jax.ShapeDtypeStruct((t, d), F32),
        compiler_params=_cparams(("arbitrary",)),
        name="ffn_dense",
    )(x2d, g, w1, w3, w2, fg)


def _ffn_moe_kernel(x_ref, g_ref, wr_ref, w1_ref, w3_ref, w2_ref, fg_ref, out_ref, hf_ref, comb_ref, acc_ref,
                    *, final):
    e = pl.program_id(1)

    @pl.when(e == 0)
    def _():
        hf = _rms(x_ref[...], g_ref[...])
        hf_ref[...] = hf.astype(BF16)
        logits = jnp.dot(hf, wr_ref[...], preferred_element_type=F32, precision=lax.Precision.HIGHEST)
        lane = _lane(logits.shape)
        lanef = lane.astype(F32)
        lg = jnp.where(lane < N_EXPERTS, logits, -jnp.inf)
        m1 = jnp.max(lg, axis=-1, keepdims=True)
        i1 = jnp.min(jnp.where(lg == m1, lanef, float(LANES)), axis=-1, keepdims=True)
        lg2 = jnp.where(lanef == i1, -jnp.inf, lg)
        m2 = jnp.max(lg2, axis=-1, keepdims=True)
        i2 = jnp.min(jnp.where(lg2 == m2, lanef, float(LANES)), axis=-1, keepdims=True)
        e2 = jnp.exp(m2 - m1)
        den = 1.0 + e2
        comb_ref[...] = jnp.where(lanef == i1, 1.0 / den, 0.0) + jnp.where(lanef == i2, e2 / den, 0.0)
        acc_ref[...] = jnp.zeros(acc_ref.shape, F32)

    comb = comb_ref[...]
    ce = jnp.sum(jnp.where(_lane(comb.shape) == e, comb, 0.0), axis=-1, keepdims=True)
    acc_ref[...] += ce * _swiglu_chunk(hf_ref[...], w1_ref[...], w3_ref[...], w2_ref[...])

    @pl.when(e == N_EXPERTS - 1)
    def _():
        y = x_ref[...] + acc_ref[...]
        out_ref[...] = _rms(y, fg_ref[...]) if final else y


def _ffn_moe(x2d, g, wr, w1, w3, w2, fg, final, tm):
    t, d = x2d.shape
    row = lambda i, e: (i, 0)
    c2 = lambda i, e: (0, 0)
    return pl.pallas_call(
        functools.partial(_ffn_moe_kernel, final=final), grid=(t // tm, N_EXPERTS),
        in_specs=[pl.BlockSpec((tm, d), row), pl.BlockSpec((1, d), c2), pl.BlockSpec((d, LANES), c2),
                  pl.BlockSpec((None, d, D_FF_EXPERT), lambda i, e: (e, 0, 0)),
                  pl.BlockSpec((None, d, D_FF_EXPERT), lambda i, e: (e, 0, 0)),
                  pl.BlockSpec((None, D_FF_EXPERT, d), lambda i, e: (e, 0, 0)),
                  pl.BlockSpec((1, d), c2)],
        out_specs=pl.BlockSpec((tm, d), row),
        out_shape=jax.ShapeDtypeStruct((t, d), F32),
        scratch_shapes=[pltpu.VMEM((tm, d), BF16), pltpu.VMEM((tm, LANES), F32), pltpu.VMEM((tm, d), F32)],
        compiler_params=_cparams(("arbitrary", "arbitrary")),
        name="ffn_moe",
    )(x2d, g, wr, w1, w3, w2, fg)


def _tile(n, want):
    t = min(n, want)
    assert n % t == 0
    return t


def kernel(x, mem, mix_norm, w_in, mla_q_norm, mla_w_uq, mla_kv_norm, mla_w_ukv, diff_lambda, diff_norm, w_branch, w_gate, b_gate, w_out, cross_norm, mem_norm, w_cq, w_ckv, w_co, ffn_norm, dense_w1, dense_w3, dense_w2, moe_router, moe_w1, moe_w3, moe_w2, final_norm):
    b, s, d = x.shape
    depth = mix_norm.shape[0]
    t = b * s
    tm = _tile(s, 512)
    bq = _tile(s, 512)
    bk = _tile(s, 1024)
    tables = _rope_tables(s)
    cols_in, cols_uq, cols_ukv = _perm_w_in(), _perm_w_uq(), _perm_w_ukv()
    row = lambda v: v.reshape(1, -1)
    fg = row(final_norm)

    xf = x.reshape(t, d)
    mem2d = mem.reshape(b * mem.shape[1], d)
    for layer in range(depth):
        lambda_init = 0.8 - 0.6 * math.exp(-0.3 * layer)
        final = layer == depth - 1
        (qta, ka, vta, qtd, kd, vtd, qtc, kc_, vtc, qtb, kb, vtb) = _proj(
            xf, row(mix_norm[layer]), w_in[layer][:, cols_in].astype(BF16), tables,
            row(mla_q_norm[layer]), mla_w_uq[layer][:, cols_uq].astype(BF16),
            row(mla_kv_norm[layer]), mla_w_ukv[layer][:, cols_ukv].astype(BF16), b, s, tm, bk, bq)
        shp = lambda a: a.reshape(b, s, a.shape[-1])
        ya = _dil_attention(qta, shp(ka), vta, bq)
        yb = _mla_attention(qtb, shp(kb), vtb, bq, bk)
        yc = _sb_attention(qtc, shp(kc_), vtc, bq, bk)
        yd = _diff_attention(qtd, shp(kd), vtd, diff_lambda[layer], row(jnp.tile(diff_norm[layer], 2)),
                             lambda_init, bq, bk)
        kc, vc = _memkv(mem2d, row(mem_norm[layer]), w_ckv[layer].astype(BF16))
        m = mem.shape[1]
        flat = lambda a: a.reshape(t, a.shape[-1])
        xf = _merge(xf, flat(ya), flat(yb), flat(yc), flat(yd), row(mix_norm[layer]),
                    w_gate[layer].astype(BF16), row(b_gate[layer]), w_branch[layer].astype(BF16),
                    w_out[layer].astype(BF16), row(cross_norm[layer]), w_cq[layer].astype(BF16),
                    kc.reshape(b, m, 256), vc.reshape(b, m, 256), w_co[layer].astype(BF16), s, tm)
        i = layer // 2
        if layer % 2 == 0:
            xf = _ffn_dense(xf, row(ffn_norm[layer]), dense_w1[i].astype(BF16), dense_w3[i].astype(BF16),
                            dense_w2[i].astype(BF16), fg, final, tm)
        else:
            wr = jnp.pad(moe_router[i], ((0, 0), (0, LANES - N_EXPERTS)))
            xf = _ffn_moe(xf, row(ffn_norm[layer]), wr, moe_w1[i].astype(BF16), moe_w3[i].astype(BF16),
                          moe_w2[i].astype(BF16), fg, final, _tile(t, 1024))
    return xf.reshape(b, s, d)
```
